```python
import math
import jax, jax.numpy as jnp
from jax import lax
import numpy as np

D_MODEL = 1024
BATCH = 32
SEQ = 2048
DEPTH = 2

N_A_LAYERS = DEPTH // 2
N_B_LAYERS = DEPTH - N_A_LAYERS
DN_ALPHA = (2 * DEPTH) ** 0.25
DN_BETA = (8 * DEPTH) ** -0.25
SSD_EXPAND = 2
SSD_D_INNER = SSD_EXPAND * D_MODEL
SSD_HEAD_DIM = 64
SSD_N_HEADS = SSD_D_INNER // SSD_HEAD_DIM
SSD_N_GROUPS = 8
SSD_HEADS_PER_GROUP = SSD_N_HEADS // SSD_N_GROUPS
SSD_D_STATE = 128
SSD_CONV = 4
SSD_CHUNK = 128
SSD_GN = SSD_N_GROUPS * SSD_D_STATE
SSD_CONV_DIM = SSD_D_INNER + 2 * SSD_GN
SSD_IN_DIM = 2 * SSD_D_INNER + 2 * SSD_GN + SSD_N_HEADS
SSD_DT_MIN = 0.001
SSD_DT_MAX = 0.1
MLA_N_HEADS = D_MODEL // 128
MLA_Q_RANK = 384
MLA_KV_RANK = 256
MLA_NOPE = 128
MLA_ROPE = 64
MLA_V = 128
ROPE_THETA = 10000.0
Q_BLOCK = 128
MAX_POS_OFFSET = 1024
FFN_HIDDEN = 2816
FFN_CONV = 3
LN_EPS = 1e-5
RMS_EPS = 1e-6

kernel_name = 'yoco_ssd_mla_convffn_deepnorm'


def layer_norm(x, g, b):
    xf = x.astype(jnp.float32)
    mu = jnp.mean(xf, axis=-1, keepdims=True)
    var = jnp.mean(jnp.square(xf - mu), axis=-1, keepdims=True)
    return ((xf - mu) * lax.rsqrt(var + LN_EPS) * g.astype(jnp.float32) + b.astype(jnp.float32)).astype(x.dtype)


def rms_norm(x, g, eps=RMS_EPS):
    xf = x.astype(jnp.float32)
    y = xf * lax.rsqrt(jnp.mean(xf * xf, axis=-1, keepdims=True) + eps)
    return (y * g.astype(jnp.float32)).astype(x.dtype)


def causal_depthwise_conv(x, w, b):
    width, ch = w.shape
    y = lax.conv_general_dilated(x, w[:, None, :].astype(x.dtype), window_strides=(1,),
                                 padding=[(width - 1, 0)],
                                 dimension_numbers=('NWC', 'WIO', 'NWC'),
                                 feature_group_count=ch)
    return y + b.astype(x.dtype)


def rope_tables(positions):
    inv_freq = 1.0 / (ROPE_THETA ** (jnp.arange(0, MLA_ROPE, 2, dtype=jnp.float32) / MLA_ROPE))
    ang = positions.astype(jnp.float32)[..., None] * inv_freq
    return jnp.cos(ang), jnp.sin(ang)


def apply_rope(x, cos, sin):
    xf = x.astype(jnp.float32)
    x1, x2 = jnp.split(xf, 2, axis=-1)
    return jnp.concatenate([x1 * cos - x2 * sin, x2 * cos + x1 * sin], axis=-1).astype(x.dtype)


def ssd_chunked_scan(xs, dt, A, Bm, Cm):
    b, s = xs.shape[:2]
    nc = s // SSD_CHUNK

    def chunks(a):
        return jnp.moveaxis(a.reshape((b, nc, SSD_CHUNK) + a.shape[2:]), 1, 0)

    causal = jnp.tril(jnp.ones((SSD_CHUNK, SSD_CHUNK), dtype=bool))[None, :, :, None, None]

    def step(state, inp):
        xc, dtc, bc, cc = inp
        cum = jnp.cumsum(dtc * A, axis=1)
        seg = cum[:, :, None] - cum[:, None, :]
        decay = jnp.exp(jnp.where(causal, seg, -jnp.inf))
        cb = jnp.einsum('btgn,bsgn->btsg', cc, bc)
        w = cb[..., None] * decay * dtc[:, None]
        y = jnp.einsum('btsgk,bsgkp->btgkp', w, xc)
        y = y + jnp.einsum('btgn,bgkpn->btgkp', cc, state) * jnp.exp(cum)[..., None]
        w_end = jnp.exp(cum[:, -1:] - cum) * dtc
        state = (state * jnp.exp(cum[:, -1])[..., None, None]
                 + jnp.einsum('bsgk,bsgkp,bsgn->bgkpn', w_end, xc, bc))
        return state, y

    init = jnp.zeros((b, SSD_N_GROUPS, SSD_HEADS_PER_GROUP, SSD_HEAD_DIM, SSD_D_STATE), jnp.float32)
    _, y = lax.scan(step, init, (chunks(xs), chunks(dt), chunks(Bm), chunks(Cm)))
    return jnp.moveaxis(y, 0, 1).reshape(xs.shape)


def ssd_mixer(x, in_proj, conv_w, conv_b, dt_bias, A_log, D, norm_g, out_proj):
    b, s, _ = x.shape
    G, K, P, N = SSD_N_GROUPS, SSD_HEADS_PER_GROUP, SSD_HEAD_DIM, SSD_D_STATE
    zxbcdt = x @ in_proj
    z = zxbcdt[..., :SSD_D_INNER]
    xbc = zxbcdt[..., SSD_D_INNER:SSD_D_INNER + SSD_CONV_DIM]
    dt = zxbcdt[..., SSD_D_INNER + SSD_CONV_DIM:]
    xbc = jax.nn.silu(causal_depthwise_conv(xbc, conv_w, conv_b))
    xs = xbc[..., :SSD_D_INNER].reshape(b, s, G, K, P).astype(jnp.float32)
    Bm = xbc[..., SSD_D_INNER:SSD_D_INNER + SSD_GN].reshape(b, s, G, N).astype(jnp.float32)
    Cm = xbc[..., SSD_D_INNER + SSD_GN:].reshape(b, s, G, N).astype(jnp.float32)
    dt = jax.nn.softplus(dt.astype(jnp.float32) + dt_bias.astype(jnp.float32)).reshape(b, s, G, K)
    A = -jnp.exp(A_log.astype(jnp.float32)).reshape(G, K)
    y = ssd_chunked_scan(xs, dt, A, Bm, Cm)
    y = y + D.astype(jnp.float32).reshape(G, K)[..., None] * xs
    y = y.reshape(b, s, SSD_D_INNER) * jax.nn.silu(z.astype(jnp.float32))
    yg = y.reshape(b, s, G, SSD_D_INNER // G)
    yg = yg * lax.rsqrt(jnp.mean(yg * yg, axis=-1, keepdims=True) + LN_EPS)
    y = (yg.reshape(b, s, SSD_D_INNER) * norm_g.astype(jnp.float32)).astype(x.dtype)
    return y @ out_proj


def mla_shared_kv(h, kv_down_proj, kv_norm_g, kv_up_k, kv_up_v, cos, sin):
    b, s, _ = h.shape
    ckv_kr = h @ kv_down_proj
    c_kv = rms_norm(ckv_kr[..., :MLA_KV_RANK], kv_norm_g)
    k_rope = apply_rope(ckv_kr[..., MLA_KV_RANK:], cos, sin)
    k_nope = (c_kv @ kv_up_k).reshape(b, s, MLA_N_HEADS, MLA_NOPE)
    v = (c_kv @ kv_up_v).reshape(b, s, MLA_N_HEADS, MLA_V)
    return k_nope, k_rope, v


def mla_attention(h, q_down, q_norm_g, q_up, out_proj, k_nope, k_rope, v, cos, sin):
    b, s, _ = h.shape
    c_q = rms_norm(h @ q_down, q_norm_g)
    q = (c_q @ q_up).reshape(b, s, MLA_N_HEADS, MLA_NOPE + MLA_ROPE)
    q_nope = q[..., :MLA_NOPE]
    q_rope = apply_rope(q[..., MLA_NOPE:], cos[:, :, None], sin[:, :, None])
    nb = s // Q_BLOCK
    scale = (MLA_NOPE + MLA_ROPE) ** -0.5
    key_idx = jnp.arange(s)

    def to_blocks(a):
        return jnp.moveaxis(a.reshape((b, nb, Q_BLOCK) + a.shape[2:]), 1, 0)

    def attend_block(args):
        qn, qr, blk = args
        scores = (jnp.einsum('bqhd,bkhd->bhqk', qn, k_nope)
                  + jnp.einsum('bqhr,bkr->bhqk', qr, k_rope)).astype(jnp.float32) * scale
        q_idx = blk * Q_BLOCK + jnp.arange(Q_BLOCK)
        scores = jnp.where(key_idx[None, :] <= q_idx[:, None], scores, -jnp.inf)
        p = jax.nn.softmax(scores, axis=-1).astype(v.dtype)
        return jnp.einsum('bhqk,bkhd->bqhd', p, v)

    o = lax.map(attend_block, (to_blocks(q_nope), to_blocks(q_rope), jnp.arange(nb)))
    o = jnp.moveaxis(o, 0, 1).reshape(b, s, MLA_N_HEADS * MLA_V)
    return o @ out_proj


def conv_ffn(h, up, conv_w, conv_b, down):
    u = causal_depthwise_conv(h @ up, conv_w, conv_b)
    g, val = jnp.split(u, 2, axis=-1)
    return (jax.nn.silu(g) * val) @ down


def setup_inputs(seed: int = 0) -> dict:
    key = jax.random.key(seed)
    ks = jax.random.split(key, 32)
    f32 = jnp.float32

    def nrm(k, shape, scale):
        return jax.random.normal(k, shape, f32) * scale

    na, nbl, d = N_A_LAYERS, N_B_LAYERS, D_MODEL
    x = nrm(ks[0], (BATCH, SEQ, d), 1.0)
    offset = jax.random.randint(ks[1], (BATCH, 1), 0, MAX_POS_OFFSET, dtype=jnp.int32)
    positions = (offset + jnp.arange(SEQ, dtype=jnp.int32)[None, :]).astype(jnp.int32)

    u = jax.random.uniform(ks[2], (na, SSD_N_HEADS), f32)
    dt0 = jnp.exp(u * (math.log(SSD_DT_MAX) - math.log(SSD_DT_MIN)) + math.log(SSD_DT_MIN))
    dt0 = jnp.maximum(dt0, 1e-4)
    ssd_dt_bias = dt0 + jnp.log(-jnp.expm1(-dt0))
    ssd_A_log = jnp.log(jax.random.uniform(ks[3], (na, SSD_N_HEADS), f32, 1.0, 16.0))

    return {
        'x': x,
        'positions': positions,
        'ssd_in_proj': nrm(ks[4], (na, d, SSD_IN_DIM), d ** -0.5),
        'ssd_conv_w': nrm(ks[5], (na, SSD_CONV, SSD_CONV_DIM), SSD_CONV ** -0.5),
        'ssd_conv_b': nrm(ks[6], (na, SSD_CONV_DIM), 0.02),
        'ssd_dt_bias': ssd_dt_bias,
        'ssd_A_log': ssd_A_log,
        'ssd_D': 1.0 + nrm(ks[7], (na, SSD_N_HEADS), 0.1),
        'ssd_norm_g': 1.0 + nrm(ks[8], (na, SSD_D_INNER), 0.02),
        'ssd_out_proj': nrm(ks[9], (na, SSD_D_INNER, d), DN_BETA * SSD_D_INNER ** -0.5),
        'kv_down_proj': nrm(ks[10], (d, MLA_KV_RANK + MLA_ROPE), d ** -0.5),
        'kv_norm_g': 1.0 + nrm(ks[11], (MLA_KV_RANK,), 0.02),
        'kv_up_k': nrm(ks[12], (MLA_KV_RANK, MLA_N_HEADS * MLA_NOPE), MLA_KV_RANK ** -0.5),
        'kv_up_v': nrm(ks[13], (MLA_KV_RANK, MLA_N_HEADS * MLA_V), DN_BETA * MLA_KV_RANK ** -0.5),
        'q_down_proj': nrm(ks[14], (nbl, d, MLA_Q_RANK), d ** -0.5),
        'q_norm_g': 1.0 + nrm(ks[15], (nbl, MLA_Q_RANK), 0.02),
        'q_up_proj': nrm(ks[16], (nbl, MLA_Q_RANK, MLA_N_HEADS * (MLA_NOPE + MLA_ROPE)), MLA_Q_RANK ** -0.5),
        'attn_out_proj': nrm(ks[17], (nbl, MLA_N_HEADS * MLA_V, d), DN_BETA * (MLA_N_HEADS * MLA_V) ** -0.5),
        'ffn_up': nrm(ks[18], (DEPTH, d, 2 * FFN_HIDDEN), d ** -0.5),
        'ffn_conv_w': nrm(ks[19], (DEPTH, FFN_CONV, 2 * FFN_HIDDEN), FFN_CONV ** -0.5),
        'ffn_conv_b': nrm(ks[20], (DEPTH, 2 * FFN_HIDDEN), 0.02),
        'ffn_down': nrm(ks[21], (DEPTH, FFN_HIDDEN, d), DN_BETA * FFN_HIDDEN ** -0.5),
        'ln_mix_g': 1.0 + nrm(ks[22], (DEPTH, d), 0.02),
        'ln_mix_b': nrm(ks[23], (DEPTH, d), 0.02),
        'ln_ffn_g': 1.0 + nrm(ks[24], (DEPTH, d), 0.02),
        'ln_ffn_b': nrm(ks[25], (DEPTH, d), 0.02),
    }


def reference(x, positions, ssd_in_proj, ssd_conv_w, ssd_conv_b, ssd_dt_bias, ssd_A_log, ssd_D,
              ssd_norm_g, ssd_out_proj, kv_down_proj, kv_norm_g, kv_up_k, kv_up_v, q_down_proj,
              q_norm_g, q_up_proj, attn_out_proj, ffn_up, ffn_conv_w, ffn_conv_b, ffn_down,
              ln_mix_g, ln_mix_b, ln_ffn_g, ln_ffn_b):
    cos, sin = rope_tables(positions)
    h = x
    shared_kv = None
    for i in range(DEPTH):
        if i < N_A_LAYERS:
            mix = ssd_mixer(h, ssd_in_proj[i], ssd_conv_w[i], ssd_conv_b[i], ssd_dt_bias[i],
                            ssd_A_log[i], ssd_D[i], ssd_norm_g[i], ssd_out_proj[i])
        else:
            j = i - N_A_LAYERS
            k_nope, k_rope, v = shared_kv
            mix = mla_attention(h, q_down_proj[j], q_norm_g[j], q_up_proj[j], attn_out_proj[j],
                                k_nope, k_rope, v, cos, sin)
        h = layer_norm(DN_ALPHA * h + mix, ln_mix_g[i], ln_mix_b[i])
        ff = conv_ffn(h, ffn_up[i], ffn_conv_w[i], ffn_conv_b[i], ffn_down[i])
        h = layer_norm(DN_ALPHA * h + ff, ln_ffn_g[i], ln_ffn_b[i])
        if i == N_A_LAYERS - 1:
            shared_kv = mla_shared_kv(h, kv_down_proj, kv_norm_g, kv_up_k, kv_up_v, cos, sin)
    return h
```

```python
import functools
import math

import jax
import jax.numpy as jnp
from jax import lax
from jax.experimental import pallas as pl
from jax.experimental.pallas import tpu as pltpu

F32 = jnp.float32
BF16 = jnp.bfloat16

DEPTH = 2
DN_ALPHA = (2 * DEPTH) ** 0.25
SSD_HEAD_DIM = 64
SSD_N_GROUPS = 8
SSD_HEADS_PER_GROUP = 4
SSD_D_STATE = 128
SSD_CONV = 4
SSD_CHUNK = 128
MLA_N_HEADS = 8
MLA_KV_RANK = 256
MLA_NOPE = 128
MLA_ROPE = 64
MLA_V = 128
ROPE_THETA = 10000.0
FFN_CONV = 3
LN_EPS = 1e-5
RMS_EPS = 1e-6

LANES = 128
SUBLANES = 8
VMEM_LIMIT_BYTES = 56 * 1024 * 1024

TOKEN_TILE = 512
Q_TILE = 256
FFN_CHUNK = 256
IN_CHUNK = 512


def _resident(shape):
    nd = len(shape)
    return pl.BlockSpec(shape, lambda *_: (0,) * nd, pipeline_mode=pl.Buffered(1))


def _params(*sem):
    return pltpu.CompilerParams(dimension_semantics=sem, vmem_limit_bytes=VMEM_LIMIT_BYTES)


def _silu(x):
    return x / (1.0 + jnp.exp(-x))


def _softplus(x):
    return jnp.maximum(x, 0.0) + jnp.log1p(jnp.exp(-jnp.abs(x)))


def _layer_norm(x, g, b):
    mu = jnp.mean(x, axis=-1, keepdims=True)
    xc = x - mu
    var = jnp.mean(xc * xc, axis=-1, keepdims=True)
    return xc * lax.rsqrt(var + LN_EPS) * g + b


def _causal_conv_rows(u, carry, w_ref, b_row, cols, width):
    ext = jnp.concatenate([carry, u], axis=0)
    y = u * w_ref[width - 1:width, cols] + b_row
    for back in range(1, width):
        shifted = pltpu.roll(ext, back, 0)[SUBLANES:]
        y = y + shifted * w_ref[width - 1 - back:width - back, cols]
    return y


def _ssd_in_kernel(x_ref, wz_ref, wxbc_ref, wdt_ref, cw_ref, cb_ref, dtb_ref,
                   zs_ref, xbc_ref, dt_ref, carry_ref):
    tm = x_ref.shape[0]

    @pl.when(pl.program_id(1) == 0)
    def _():
        carry_ref[...] = jnp.zeros_like(carry_ref)

    xb = x_ref[...].astype(BF16)
    for c in range(wz_ref.shape[1] // IN_CHUNK):
        cols = slice(c * IN_CHUNK, (c + 1) * IN_CHUNK)
        z = jnp.dot(xb, wz_ref[:, cols], preferred_element_type=F32)
        zs_ref[:, cols] = _silu(z).astype(BF16)
    for c in range(wxbc_ref.shape[1] // IN_CHUNK):
        cols = slice(c * IN_CHUNK, (c + 1) * IN_CHUNK)
        u = jnp.dot(xb, wxbc_ref[:, cols], preferred_element_type=F32)
        y = _causal_conv_rows(u, carry_ref[:, cols], cw_ref, cb_ref[:, cols], cols, SSD_CONV)
        carry_ref[:, cols] = u[tm - SUBLANES:, :]
        xbc_ref[:, cols] = _silu(y).astype(BF16)
    dt = jnp.dot(xb, wdt_ref[...], preferred_element_type=F32) + dtb_ref[...]
    dt_ref[...] = _softplus(dt)


def _ssd_in(x2, wz, wxbc, wdt, cw, cb, dtb, batch, seq):
    t, d = x2.shape
    tm = min(TOKEN_TILE, seq)
    nt = seq // tm
    n_z, n_xbc = wz.shape[1], wxbc.shape[1]
    row = lambda b, i: (b * nt + i, 0)
    return pl.pallas_call(
        _ssd_in_kernel,
        grid=(batch, nt),
        in_specs=[pl.BlockSpec((tm, d), row), _resident(wz.shape), _resident(wxbc.shape),
                  _resident(wdt.shape), _resident(cw.shape), _resident(cb.shape), _resident(dtb.shape)],
        out_specs=[pl.BlockSpec((tm, n_z), row), pl.BlockSpec((tm, n_xbc), row),
                   pl.BlockSpec((tm, LANES), row)],
        out_shape=[jax.ShapeDtypeStruct((t, n_z), BF16), jax.ShapeDtypeStruct((t, n_xbc), BF16),
                   jax.ShapeDtypeStruct((t, LANES), F32)],
        scratch_shapes=[pltpu.VMEM((SUBLANES, n_xbc), F32)],
        compiler_params=_params("arbitrary", "arbitrary"),
        name="ssd_in",
    )(x2, wz, wxbc, wdt, cw, cb, dtb)


def _split3(a):
    hi = a.astype(BF16)
    r = a - hi.astype(F32)
    mid = r.astype(BF16)
    lo = (r - mid.astype(F32)).astype(BF16)
    return hi, mid, lo


def _ssd_scan_kernel(xbc_ref, zs_ref, dt_ref, alog_ref, dexp_ref, ng_ref, y_ref, state_ref):
    L = SSD_CHUNK
    G = SSD_N_GROUPS
    N = SSD_D_STATE
    P = SSD_HEAD_DIM
    hpg = SSD_HEADS_PER_GROUP
    gw = hpg * P
    d_inner = G * gw

    @pl.when(pl.program_id(1) == 0)
    def _():
        state_ref[...] = jnp.zeros_like(state_ref)

    dt = dt_ref[...]
    a_row = -jnp.exp(alog_ref[...])
    d_a = dt * a_row
    rows = lax.broadcasted_iota(jnp.int32, (L, L), 0)
    cols = lax.broadcasted_iota(jnp.int32, (L, L), 1)
    causal = cols <= rows
    tri = jnp.where(causal, 1.0, 0.0).astype(BF16)
    hi, mid, lo = _split3(d_a)
    cum = (jnp.dot(tri, hi, preferred_element_type=F32)
           + jnp.dot(tri, mid, preferred_element_type=F32)
           + jnp.dot(tri, lo, preferred_element_type=F32))
    cum_t = cum.T
    dt_t = dt.T
    lane = lax.broadcasted_iota(jnp.int32, (L, LANES), 1)
    low_half = lane < P
    lane_g = lax.broadcasted_iota(jnp.int32, (L, gw), 1)

    for g in range(G):
        xg = xbc_ref[:, g * gw:(g + 1) * gw]
        bg = xbc_ref[:, d_inner + g * N:d_inner + (g + 1) * N]
        cg = xbc_ref[:, d_inner + G * N + g * N:d_inner + G * N + (g + 1) * N]
        cb = lax.dot_general(cg, bg, (((1,), (1,)), ((), ())), preferred_element_type=F32)

        w_parts, x_parts, cum_b, dt_b = [], [], [], []
        for j in range(hpg):
            h = g * hpg + j
            ccol = jnp.broadcast_to(cum[:, h:h + 1], (L, LANES))
            dcol = jnp.broadcast_to(dt[:, h:h + 1], (L, LANES))
            cum_b.append(ccol)
            dt_b.append(dcol)
            seg = ccol - cum_t[h:h + 1, :]
            decay = jnp.exp(jnp.where(causal, seg, -jnp.inf))
            w_parts.append((cb * decay * dt_t[h:h + 1, :]).astype(BF16))
            in_head = (lane_g >= j * P) & (lane_g < (j + 1) * P)
            x_parts.append(jnp.where(in_head, xg, jnp.zeros_like(xg)))
        w_cat = jnp.concatenate(w_parts, axis=1)
        x_bd = jnp.concatenate(x_parts, axis=0)
        y_diag = jnp.dot(w_cat, x_bd, preferred_element_type=F32)

        state = state_ref[g]
        y_off = jnp.dot(cg, state.astype(BF16), preferred_element_type=F32)

        cum_e = jnp.concatenate([jnp.where(low_half, cum_b[0], cum_b[1]),
                                 jnp.where(low_half, cum_b[2], cum_b[3])], axis=1)
        dt_e = jnp.concatenate([jnp.where(low_half, dt_b[0], dt_b[1]),
                                jnp.where(low_half, dt_b[2], dt_b[3])], axis=1)
        last = cum_e[L - 1:L, :]
        xg32 = xg.astype(F32)
        y = y_diag + y_off * jnp.exp(cum_e)

        w_end = jnp.exp(last - cum_e) * dt_e
        xw = (xg32 * w_end).astype(BF16)
        upd = lax.dot_general(bg, xw, (((0,), (0,)), ((), ())), preferred_element_type=F32)
        state_ref[g] = state * jnp.exp(last) + upd

        cols_g = slice(g * gw, (g + 1) * gw)
        y = y + dexp_ref[:, cols_g] * xg32
        y = y * zs_ref[:, cols_g].astype(F32)
        ms = jnp.mean(y * y, axis=-1, keepdims=True)
        y = y * lax.rsqrt(ms + LN_EPS) * ng_ref[:, cols_g]
        y_ref[:, cols_g] = y.astype(BF16)


def _ssd_scan(xbc, zs, dt, alog, dexp, ng, batch, seq):
    t = xbc.shape[0]
    nc = seq // SSD_CHUNK
    d_inner = zs.shape[1]
    row = lambda b, c: (b * nc + c, 0)
    return pl.pallas_call(
        _ssd_scan_kernel,
        grid=(batch, nc),
        in_specs=[pl.BlockSpec((SSD_CHUNK, xbc.shape[1]), row), pl.BlockSpec((SSD_CHUNK, d_inner), row),
                  pl.BlockSpec((SSD_CHUNK, LANES), row), _resident(alog.shape), _resident(dexp.shape),
                  _resident(ng.shape)],
        out_specs=pl.BlockSpec((SSD_CHUNK, d_inner), row),
        out_shape=jax.ShapeDtypeStruct((t, d_inner), BF16),
        scratch_shapes=[pltpu.VMEM((SSD_N_GROUPS, SSD_D_STATE, d_inner // SSD_N_GROUPS), F32)],
        compiler_params=_params("arbitrary", "arbitrary"),
        name="ssd_scan",
    )(xbc, zs, dt, alog, dexp, ng)


def _proj_ln_kernel(a_ref, w_ref, res_ref, g_ref, b_ref, o_ref):
    mix = jnp.dot(a_ref[...], w_ref[...], preferred_element_type=F32)
    o_ref[...] = _layer_norm(DN_ALPHA * res_ref[...] + mix, g_ref[...], b_ref[...])


def _proj_ln(a, w, res, g, b):
    t, k = a.shape
    d = w.shape[1]
    tm = min(TOKEN_TILE, t)
    row = lambda i: (i, 0)
    return pl.pallas_call(
        _proj_ln_kernel,
        grid=(t // tm,),
        in_specs=[pl.BlockSpec((tm, k), row), _resident(w.shape), pl.BlockSpec((tm, d), row),
                  _resident(g.shape), _resident(b.shape)],
        out_specs=pl.BlockSpec((tm, d), row),
        out_shape=jax.ShapeDtypeStruct((t, d), F32),
        compiler_params=_params("arbitrary"),
        name="proj_ln",
    )(a, w, res, g, b)


def _ffn_kernel(h_ref, up_ref, cw_ref, cb_ref, down_ref, g_ref, b_ref, o_ref, carry_ref):
    tm = h_ref.shape[0]
    hidden = down_ref.shape[0]

    @pl.when(pl.program_id(1) == 0)
    def _():
        carry_ref[...] = jnp.zeros_like(carry_ref)

    h = h_ref[...]
    hb = h.astype(BF16)
    acc = jnp.zeros((tm, down_ref.shape[1]), F32)
    for c in range(hidden // FFN_CHUNK):
        act = None
        for half in range(2):
            cols = slice(half * hidden + c * FFN_CHUNK, half * hidden + (c + 1) * FFN_CHUNK)
            u = jnp.dot(hb, up_ref[:, cols], preferred_element_type=F32)
            y = _causal_conv_rows(u, carry_ref[:, cols], cw_ref, cb_ref[:, cols], cols, FFN_CONV)
            carry_ref[:, cols] = u[tm - SUBLANES:, :]
            act = _silu(y) if half == 0 else act * y
        acc = acc + jnp.dot(act.astype(BF16), down_ref[c * FFN_CHUNK:(c + 1) * FFN_CHUNK, :],
                            preferred_element_type=F32)
    o_ref[...] = _layer_norm(DN_ALPHA * h + acc, g_ref[...], b_ref[...])


def _ffn_ln(h, up, cw, cb, down, g, b, batch, seq):
    t, d = h.shape
    tm = min(TOKEN_TILE, seq)
    nt = seq // tm
    row = lambda bi, i: (bi * nt + i, 0)
    return pl.pallas_call(
        _ffn_kernel,
        grid=(batch, nt),
        in_specs=[pl.BlockSpec((tm, d), row), _resident(up.shape), _resident(cw.shape), _resident(cb.shape),
                  _resident(down.shape), _resident(g.shape), _resident(b.shape)],
        out_specs=pl.BlockSpec((tm, d), row),
        out_shape=jax.ShapeDtypeStruct((t, d), F32),
        scratch_shapes=[pltpu.VMEM((SUBLANES, up.shape[1]), F32)],
        compiler_params=_params("arbitrary", "arbitrary"),
        name="ffn_ln",
    )(h, up, cw, cb, down, g, b)


def _rope_kernel(pos_ref, invf_ref, cos_ref, sin_ref):
    ang = pos_ref[...].astype(F32) * invf_ref[0]
    cos_ref[0] = jnp.cos(ang)
    sin_ref[0] = jnp.sin(ang)


def _rope_tables(pos2, invf):
    nf = invf.shape[0]
    r = pos2.shape[0]
    shape = jax.ShapeDtypeStruct((nf, r, LANES), F32)
    blk = pl.BlockSpec((1, r, LANES), lambda f: (f, 0, 0))
    return pl.pallas_call(
        _rope_kernel,
        grid=(nf,),
        in_specs=[pl.BlockSpec((r, LANES), lambda f: (0, 0)), pl.BlockSpec((1, 1, LANES), lambda f: (f, 0, 0))],
        out_specs=[blk, blk],
        out_shape=[shape, shape],
        compiler_params=_params("arbitrary"),
        name="rope_tables",
    )(pos2, invf)


def _rope_combine(t):
    return t + pltpu.roll(t, MLA_ROPE, 1)


def _mla_kv_kernel(h_ref, wd_ref, ng_ref, wk_ref, wv_ref, cs_ref, kn_ref, kr_ref, v_ref):
    hb = h_ref[...].astype(BF16)
    ckv = jnp.dot(hb, wd_ref[...], preferred_element_type=F32)
    c = ckv[:, :MLA_KV_RANK]
    c = c * lax.rsqrt(jnp.mean(c * c, axis=-1, keepdims=True) + RMS_EPS) * ng_ref[...]
    cb = c.astype(BF16)
    kn_ref[...] = jnp.dot(cb, wk_ref[...], preferred_element_type=F32).astype(BF16)
    v_ref[...] = jnp.dot(cb, wv_ref[...], preferred_element_type=F32).astype(BF16)
    kr = _rope_combine(ckv[:, MLA_KV_RANK:] * cs_ref[...])
    lane = lax.broadcasted_iota(jnp.int32, kr.shape, 1)
    kr_ref[...] = jnp.where(lane < MLA_ROPE, kr, 0.0).astype(BF16)


def _mla_kv(h, wd, ng, wk, wv, cs):
    t, d = h.shape
    tm = min(TOKEN_TILE, t)
    row = lambda i: (i, 0)
    n = wk.shape[1]
    return pl.pallas_call(
        _mla_kv_kernel,
        grid=(t // tm,),
        in_specs=[pl.BlockSpec((tm, d), row), _resident(wd.shape), _resident(ng.shape), _resident(wk.shape),
                  _resident(wv.shape), pl.BlockSpec((tm, LANES), row)],
        out_specs=[pl.BlockSpec((tm, n), row), pl.BlockSpec((tm, LANES), row), pl.BlockSpec((tm, n), row)],
        out_shape=[jax.ShapeDtypeStruct((t, n), BF16), jax.ShapeDtypeStruct((t, LANES), BF16),
                   jax.ShapeDtypeStruct((t, n), BF16)],
        compiler_params=_params("arbitrary"),
        name="mla_kv",
    )(h, wd, ng, wk, wv, cs)


def _mla_q_kernel(h_ref, wd_ref, ng_ref, wu_ref, cs_ref, qn_ref, qr_ref):
    scale = (MLA_NOPE + MLA_ROPE) ** -0.5
    hb = h_ref[...].astype(BF16)
    c = jnp.dot(hb, wd_ref[...], preferred_element_type=F32)
    c = c * lax.rsqrt(jnp.mean(c * c, axis=-1, keepdims=True) + RMS_EPS) * ng_ref[...]
    cb = c.astype(BF16)
    cs = cs_ref[...]
    for hd in range(MLA_N_HEADS):
        q = jnp.dot(cb, wu_ref[:, hd * 2 * LANES:(hd + 1) * 2 * LANES], preferred_element_type=F32)
        out = slice(hd * LANES, (hd + 1) * LANES)
        qn_ref[:, out] = (q[:, :LANES] * scale).astype(BF16)
        qr_ref[:, out] = (_rope_combine(q[:, LANES:] * cs) * scale).astype(BF16)


def _mla_q(h, wd, ng, wu, cs):
    t, d = h.shape
    tm = min(TOKEN_TILE, t)
    row = lambda i: (i, 0)
    n = MLA_N_HEADS * LANES
    return pl.pallas_call(
        _mla_q_kernel,
        grid=(t // tm,),
        in_specs=[pl.BlockSpec((tm, d), row), _resident(wd.shape), _resident(ng.shape), _resident(wu.shape),
                  pl.BlockSpec((tm, LANES), row)],
        out_specs=[pl.BlockSpec((tm, n), row), pl.BlockSpec((tm, n), row)],
        out_shape=[jax.ShapeDtypeStruct((t, n), BF16), jax.ShapeDtypeStruct((t, n), BF16)],
        compiler_params=_params("arbitrary"),
        name="mla_q",
    )(h, wd, ng, wu, cs)


def _attn_kernel(qn_ref, qr_ref, kn_ref, kr_ref, v_ref, o_ref):
    seq = qn_ref.shape[0]
    tq = min(Q_TILE, seq)
    nt_dims = (((1,), (1,)), ((), ()))
    r = lax.broadcasted_iota(jnp.int32, (tq, tq), 0)
    c = lax.broadcasted_iota(jnp.int32, (tq, tq), 1)
    diag_mask = c <= r
    for qi in range(seq // tq):
        lo = qi * tq
        q = jnp.concatenate([qn_ref[lo:lo + tq, :], qr_ref[lo:lo + tq, :]], axis=1)
        k_d = jnp.concatenate([kn_ref[lo:lo + tq, :], kr_ref[lo:lo + tq, :]], axis=1)
        s_d = lax.dot_general(q, k_d, nt_dims, preferred_element_type=F32)
        s_d = jnp.where(diag_mask, s_d, -jnp.inf)
        m = jnp.max(s_d, axis=-1, keepdims=True)
        if qi > 0:
            k_p = jnp.concatenate([kn_ref[:lo, :], kr_ref[:lo, :]], axis=1)
            s_p = lax.dot_general(q, k_p, nt_dims, preferred_element_type=F32)
            m = jnp.maximum(m, jnp.max(s_p, axis=-1, keepdims=True))
        p_d = jnp.exp(s_d - m)
        den = jnp.sum(p_d, axis=-1, keepdims=True)
        acc = jnp.dot(p_d.astype(BF16), v_ref[lo:lo + tq, :], preferred_element_type=F32)
        if qi > 0:
            p_p = jnp.exp(s_p - m)
            den = den + jnp.sum(p_p, axis=-1, keepdims=True)
            acc = acc + jnp.dot(p_p.astype(BF16), v_ref[:lo, :], preferred_element_type=F32)
        o_ref[lo:lo + tq, :] = (acc / den).astype(BF16)


def _attention(qn, qr, kn, kr, v, batch, seq):
    t = qn.shape[0]
    head = lambda b, h: (b, h)
    blk = pl.BlockSpec((seq, LANES), head)
    return pl.pallas_call(
        _attn_kernel,
        grid=(batch, MLA_N_HEADS),
        in_specs=[blk, blk, blk, pl.BlockSpec((seq, LANES), lambda b, h: (b, 0)), blk],
        out_specs=blk,
        out_shape=jax.ShapeDtypeStruct((t, MLA_N_HEADS * MLA_V), BF16),
        compiler_params=_params("arbitrary", "arbitrary"),
        name="attn",
    )(qn, qr, kn, kr, v)


def _swap_halves(w):
    half = w.shape[-1] // 2
    return jnp.concatenate([w[..., half:], w[..., :half]], axis=-1)


def _row(v):
    return v.reshape(1, -1).astype(F32)


def kernel(x, positions, ssd_in_proj, ssd_conv_w, ssd_conv_b, ssd_dt_bias, ssd_A_log, ssd_D, ssd_norm_g, ssd_out_proj, kv_down_proj, kv_norm_g, kv_up_k, kv_up_v, q_down_proj, q_norm_g, q_up_proj, attn_out_proj, ffn_up, ffn_conv_w, ffn_conv_b, ffn_down, ln_mix_g, ln_mix_b, ln_ffn_g, ln_ffn_b):
    batch, seq, d = x.shape
    t = batch * seq
    n_a = ssd_in_proj.shape[0]
    n_layers = ffn_up.shape[0]
    d_inner = ssd_norm_g.shape[1]
    n_heads = ssd_A_log.shape[1]
    assert seq % SSD_CHUNK == 0 and seq % min(TOKEN_TILE, seq) == 0 and t % LANES == 0
    assert n_heads <= LANES and d_inner == n_heads * SSD_HEAD_DIM

    inv_freq = 1.0 / (ROPE_THETA ** (jnp.arange(0, MLA_ROPE, 2, dtype=F32) / MLA_ROPE))
    invf = jnp.broadcast_to(inv_freq[:, None, None], (MLA_ROPE // 2, 1, LANES))
    cos_t, sin_t = _rope_tables(positions.reshape(t // LANES, LANES), invf)
    cos = cos_t.reshape(MLA_ROPE // 2, t).T
    sin = sin_t.reshape(MLA_ROPE // 2, t).T
    cs = jnp.concatenate([cos, cos, -sin, sin], axis=1)

    h = x.reshape(t, d)
    kn = kr = v = None
    for i in range(n_layers):
        if i < n_a:
            w_in = ssd_in_proj[i].astype(BF16)
            n_xbc = ssd_conv_w.shape[2]
            wz = w_in[:, :d_inner]
            wxbc = w_in[:, d_inner:d_inner + n_xbc]
            wdt = jnp.pad(w_in[:, d_inner + n_xbc:], ((0, 0), (0, LANES - n_heads)))
            dtb = jnp.pad(ssd_dt_bias[i].astype(F32), (0, LANES - n_heads)).reshape(1, LANES)
            alog = jnp.pad(ssd_A_log[i].astype(F32), (0, LANES - n_heads)).reshape(1, LANES)
            zs, xbc, dt = _ssd_in(h, wz, wxbc, wdt, ssd_conv_w[i].astype(F32), _row(ssd_conv_b[i]), dtb,
                                  batch, seq)
            dexp = jnp.repeat(ssd_D[i].astype(F32), SSD_HEAD_DIM).reshape(1, d_inner)
            y = _ssd_scan(xbc, zs, dt, alog, dexp, _row(ssd_norm_g[i]), batch, seq)
            h = _proj_ln(y, ssd_out_proj[i].astype(BF16), h, _row(ln_mix_g[i]), _row(ln_mix_b[i]))
        else:
            j = i - n_a
            qu = q_up_proj[j].reshape(-1, MLA_N_HEADS, MLA_NOPE + MLA_ROPE)
            qu_ext = jnp.concatenate([qu, _swap_halves(qu[..., MLA_NOPE:])], axis=-1)
            qu_ext = qu_ext.reshape(qu.shape[0], MLA_N_HEADS * 2 * LANES).astype(BF16)
            qn, qr = _mla_q(h, q_down_proj[j].astype(BF16), _row(q_norm_g[j]), qu_ext, cs)
            o = _attention(qn, qr, kn, kr, v, batch, seq)
            h = _proj_ln(o, attn_out_proj[j].astype(BF16), h, _row(ln_mix_g[i]), _row(ln_mix_b[i]))
        h = _ffn_ln(h, ffn_up[i].astype(BF16), ffn_conv_w[i].astype(F32), _row(ffn_conv_b[i]),
                    ffn_down[i].astype(BF16), _row(ln_ffn_g[i]), _row(ln_ffn_b[i]), batch, seq)
        if i == n_a - 1:
            wd_ext = jnp.concatenate([kv_down_proj, _swap_halves(kv_down_proj[:, MLA_KV_RANK:])], axis=1)
            kn, kr, v = _mla_kv(h, wd_ext.astype(BF16), _row(kv_norm_g), kv_up_k.astype(BF16),
                                kv_up_v.astype(BF16), cs)
    return h.reshape(batch, seq, d)
```

```python
import functools
import math

import jax
import jax.numpy as jnp
from jax import lax
from jax.experimental import pallas as pl
from jax.experimental.pallas import tpu as pltpu

F32 = jnp.float32
BF16 = jnp.bfloat16

DEPTH = 2
DN_ALPHA = (2 * DEPTH) ** 0.25
SSD_HEAD_DIM = 64
SSD_N_GROUPS = 8
SSD_HEADS_PER_GROUP = 4
SSD_D_STATE = 128
SSD_CONV = 4
SSD_CHUNK = 128
MLA_N_HEADS = 8
MLA_KV_RANK = 256
MLA_NOPE = 128
MLA_ROPE = 64
MLA_V = 128
ROPE_THETA = 10000.0
FFN_CONV = 3
LN_EPS = 1e-5
RMS_EPS = 1e-6

LANES = 128
SUBLANES = 8
VMEM_LIMIT_BYTES = 56 * 1024 * 1024

TOKEN_TILE = 512
Q_TILE = 256
FFN_CHUNK = 256
IN_CHUNK = 512


def _resident(shape):
    nd = len(shape)
    return pl.BlockSpec(shape, lambda *_: (0,) * nd, pipeline_mode=pl.Buffered(1))


def _params(*sem):
    return pltpu.CompilerParams(dimension_semantics=sem, vmem_limit_bytes=VMEM_LIMIT_BYTES)


def _silu(x):
    h = 0.5 * x
    return h + h * jnp.tanh(h)


def _softplus(x):
    return jnp.maximum(x, 0.0) + jnp.log1p(jnp.exp(-jnp.abs(x)))


def _layer_norm(x, g, b):
    mu = jnp.mean(x, axis=-1, keepdims=True)
    xc = x - mu
    var = jnp.mean(xc * xc, axis=-1, keepdims=True)
    return xc * lax.rsqrt(var + LN_EPS) * g + b


def _shift_rows(t, prev_last_row):
    rows, n = t.shape
    first = lax.broadcasted_iota(jnp.int32, (SUBLANES, n), 0) == 0
    rot_prev = jnp.broadcast_to(prev_last_row, (SUBLANES, n))
    out = []
    for i in range(rows // SUBLANES):
        rot = pltpu.roll(t[i * SUBLANES:(i + 1) * SUBLANES, :], 1, 0)
        out.append(jnp.where(first, rot_prev, rot))
        rot_prev = rot
    return jnp.concatenate(out, axis=0)


def _causal_conv_rows(u, carry, w, b_row, width):
    tm = u.shape[0]
    t = u * w[0:1]
    new_carry = []
    for k in range(1, width):
        new_carry.append(t[tm - 1:tm, :])
        t = _shift_rows(t, carry[k - 1:k, :]) + u * w[k:k + 1]
    return t + b_row, new_carry


def _lane_chunk(c, width):
    start = c * width
    return pl.ds(start if isinstance(start, int) else pl.multiple_of(start, width), width)


def _ssd_in_kernel(x_ref, w_ref, wdt_ref, cw_ref, cb_ref, dtb_ref,
                   zs_ref, xbc_ref, dt_ref, carry_ref, xb_ref, u_ref):
    n_z = zs_ref.shape[1] // IN_CHUNK
    n_x = xbc_ref.shape[1] // IN_CHUNK
    assert n_z % 2 == 0 and n_x % 2 == 0

    @pl.when(pl.program_id(1) == 0)
    def _():
        carry_ref[...] = jnp.zeros_like(carry_ref)

    xb_ref[...] = x_ref[...].astype(BF16)

    def project(c, slot):
        u_ref[slot] = jnp.dot(xb_ref[...], w_ref[c], preferred_element_type=F32)

    def gate_z(c, slot):
        zs_ref[:, _lane_chunk(c, IN_CHUNK)] = _silu(u_ref[slot]).astype(BF16)

    def conv_x(c, slot):
        j = c - n_z
        y, rows = _causal_conv_rows(u_ref[slot], carry_ref[j], cw_ref[j], cb_ref[j], SSD_CONV)
        for k, r in enumerate(rows):
            carry_ref[j, k:k + 1, :] = r
        xbc_ref[:, _lane_chunk(j, IN_CHUNK)] = _silu(y).astype(BF16)

    def body_z(p, _):
        c = 2 * p
        project(c + 1, 1)
        gate_z(c, 0)
        project(c + 2, 0)
        gate_z(c + 1, 1)
        return 0

    def body_x(p, _):
        c = n_z + 2 * p
        project(c + 1, 1)
        conv_x(c, 0)
        project(c + 2, 0)
        conv_x(c + 1, 1)
        return 0

    project(0, 0)
    lax.fori_loop(0, n_z // 2, body_z, 0)
    lax.fori_loop(0, n_x // 2 - 1, body_x, 0)
    c = n_z + n_x - 2
    project(c + 1, 1)
    conv_x(c, 0)
    conv_x(c + 1, 1)
    dt = jnp.dot(xb_ref[...], wdt_ref[...], preferred_element_type=F32) + dtb_ref[...]
    dt_ref[...] = _softplus(dt)


def _ssd_in(x2, wz, wxbc, wdt, cw, cb, dtb, batch, seq):
    t, d = x2.shape
    tm = min(TOKEN_TILE, seq)
    nt = seq // tm
    n_z, n_xbc = wz.shape[1], wxbc.shape[1]
    nc_x = n_xbc // IN_CHUNK
    w3 = jnp.concatenate([wz, wxbc], axis=1).reshape(d, (n_z + n_xbc) // IN_CHUNK, IN_CHUNK).transpose(1, 0, 2)
    cw3 = cw.reshape(SSD_CONV, nc_x, IN_CHUNK).transpose(1, 0, 2)
    cb3 = cb.reshape(nc_x, 1, IN_CHUNK)
    row = lambda b, i: (b * nt + i, 0)
    return pl.pallas_call(
        _ssd_in_kernel,
        grid=(batch, nt),
        in_specs=[pl.BlockSpec((tm, d), row), _resident(w3.shape), _resident(wdt.shape), _resident(cw3.shape),
                  _resident(cb3.shape), _resident(dtb.shape)],
        out_specs=[pl.BlockSpec((tm, n_z), row), pl.BlockSpec((tm, n_xbc), row),
                   pl.BlockSpec((tm, LANES), row)],
        out_shape=[jax.ShapeDtypeStruct((t, n_z), BF16), jax.ShapeDtypeStruct((t, n_xbc), BF16),
                   jax.ShapeDtypeStruct((t, LANES), F32)],
        scratch_shapes=[pltpu.VMEM((nc_x, SUBLANES, IN_CHUNK), F32), pltpu.VMEM((tm, d), BF16),
                        pltpu.VMEM((2, tm, IN_CHUNK), F32)],
        compiler_params=_params("arbitrary", "arbitrary"),
        name="ssd_in",
    )(x2, w3, wdt, cw3, cb3, dtb)


def _split3(a):
    hi = a.astype(BF16)
    r = a - hi.astype(F32)
    mid = r.astype(BF16)
    lo = (r - mid.astype(F32)).astype(BF16)
    return hi, mid, lo


def _ssd_scan_kernel(xbc_ref, zs_ref, dt_ref, alog_ref, dexp_ref, ng_ref, y_ref, state_ref):
    L = SSD_CHUNK
    G = SSD_N_GROUPS
    N = SSD_D_STATE
    P = SSD_HEAD_DIM
    hpg = SSD_HEADS_PER_GROUP
    gw = hpg * P
    d_inner = G * gw

    @pl.when(pl.program_id(1) == 0)
    def _():
        state_ref[...] = jnp.zeros_like(state_ref)

    dt = dt_ref[...]
    a_row = -jnp.exp(alog_ref[...])
    d_a = dt * a_row
    rows = lax.broadcasted_iota(jnp.int32, (L, L), 0)
    cols = lax.broadcasted_iota(jnp.int32, (L, L), 1)
    causal = cols <= rows
    tri = jnp.where(causal, 1.0, 0.0).astype(BF16)
    hi, mid, lo = _split3(d_a)
    cum = (jnp.dot(tri, hi, preferred_element_type=F32)
           + jnp.dot(tri, mid, preferred_element_type=F32)
           + jnp.dot(tri, lo, preferred_element_type=F32))
    cum_t = cum.T
    dt_t = dt.T
    lane = lax.broadcasted_iota(jnp.int32, (L, LANES), 1)
    low_half = lane < P
    lane_g = lax.broadcasted_iota(jnp.int32, (L, gw), 1)

    for g in range(G):
        xg = xbc_ref[:, g * gw:(g + 1) * gw]
        bg = xbc_ref[:, d_inner + g * N:d_inner + (g + 1) * N]
        cg = xbc_ref[:, d_inner + G * N + g * N:d_inner + G * N + (g + 1) * N]
        cb = lax.dot_general(cg, bg, (((1,), (1,)), ((), ())), preferred_element_type=F32)

        w_parts, x_parts, cum_b, dt_b = [], [], [], []
        for j in range(hpg):
            h = g * hpg + j
            ccol = jnp.broadcast_to(cum[:, h:h + 1], (L, LANES))
            dcol = jnp.broadcast_to(dt[:, h:h + 1], (L, LANES))
            cum_b.append(ccol)
            dt_b.append(dcol)
            seg = ccol - cum_t[h:h + 1, :]
            decay = jnp.exp(jnp.where(causal, seg, -jnp.inf))
            w_parts.append((cb * decay * dt_t[h:h + 1, :]).astype(BF16))
            in_head = (lane_g >= j * P) & (lane_g < (j + 1) * P)
            x_parts.append(jnp.where(in_head, xg, jnp.zeros_like(xg)))
        w_cat = jnp.concatenate(w_parts, axis=1)
        x_bd = jnp.concatenate(x_parts, axis=0)
        y_diag = jnp.dot(w_cat, x_bd, preferred_element_type=F32)

        state = state_ref[g]
        y_off = jnp.dot(cg, state.astype(BF16), preferred_element_type=F32)

        cum_e = jnp.concatenate([jnp.where(low_half, cum_b[0], cum_b[1]),
                                 jnp.where(low_half, cum_b[2], cum_b[3])], axis=1)
        dt_e = jnp.concatenate([jnp.where(low_half, dt_b[0], dt_b[1]),
                                jnp.where(low_half, dt_b[2], dt_b[3])], axis=1)
        last = cum_e[L - 1:L, :]
        xg32 = xg.astype(F32)
        y = y_diag + y_off * jnp.exp(cum_e)

        w_end = jnp.exp(last - cum_e) * dt_e
        xw = (xg32 * w_end).astype(BF16)
        upd = lax.dot_general(bg, xw, (((0,), (0,)), ((), ())), preferred_element_type=F32)
        state_ref[g] = state * jnp.exp(last) + upd

        cols_g = slice(g * gw, (g + 1) * gw)
        y = y + dexp_ref[:, cols_g] * xg32
        y = y * zs_ref[:, cols_g].astype(F32)
        ms = jnp.mean(y * y, axis=-1, keepdims=True)
        y = y * lax.rsqrt(ms + LN_EPS) * ng_ref[:, cols_g]
        y_ref[:, cols_g] = y.astype(BF16)


def _ssd_scan(xbc, zs, dt, alog, dexp, ng, batch, seq):
    t = xbc.shape[0]
    nc = seq // SSD_CHUNK
    d_inner = zs.shape[1]
    row = lambda b, c: (b * nc + c, 0)
    return pl.pallas_call(
        _ssd_scan_kernel,
        grid=(batch, nc),
        in_specs=[pl.BlockSpec((SSD_CHUNK, xbc.shape[1]), row), pl.BlockSpec((SSD_CHUNK, d_inner), row),
                  pl.BlockSpec((SSD_CHUNK, LANES), row), _resident(alog.shape), _resident(dexp.shape),
                  _resident(ng.shape)],
        out_specs=pl.BlockSpec((SSD_CHUNK, d_inner), row),
        out_shape=jax.ShapeDtypeStruct((t, d_inner), BF16),
        scratch_shapes=[pltpu.VMEM((SSD_N_GROUPS, SSD_D_STATE, d_inner // SSD_N_GROUPS), F32)],
        compiler_params=_params("arbitrary", "arbitrary"),
        name="ssd_scan",
    )(xbc, zs, dt, alog, dexp, ng)


def _proj_ln_kernel(a_ref, w_ref, res_ref, g_ref, b_ref, o_ref):
    mix = jnp.dot(a_ref[...], w_ref[...], preferred_element_type=F32)
    o_ref[...] = _layer_norm(DN_ALPHA * res_ref[...] + mix, g_ref[...], b_ref[...])


def _proj_ln(a, w, res, g, b):
    t, k = a.shape
    d = w.shape[1]
    tm = min(TOKEN_TILE, t)
    row = lambda i: (i, 0)
    return pl.pallas_call(
        _proj_ln_kernel,
        grid=(t // tm,),
        in_specs=[pl.BlockSpec((tm, k), row), _resident(w.shape), pl.BlockSpec((tm, d), row),
                  _resident(g.shape), _resident(b.shape)],
        out_specs=pl.BlockSpec((tm, d), row),
        out_shape=jax.ShapeDtypeStruct((t, d), F32),
        compiler_params=_params("arbitrary"),
        name="proj_ln",
    )(a, w, res, g, b)


def _ffn_kernel(h_ref, up_ref, cw_ref, cb_ref, down_ref, g_ref, b_ref, o_ref,
                carry_ref, hb_ref, u_ref, act_ref):
    n = up_ref.shape[0] // 2
    assert (n - 1) % 2 == 0

    @pl.when(pl.program_id(1) == 0)
    def _():
        carry_ref[...] = jnp.zeros_like(carry_ref)

    hb_ref[...] = h_ref[...].astype(BF16)

    def project(c, slot):
        for half in range(2):
            u_ref[slot * 2 + half] = jnp.dot(hb_ref[...], up_ref[half * n + c], preferred_element_type=F32)

    def gate(c, slot):
        act = None
        for half in range(2):
            idx = half * n + c
            y, rows = _causal_conv_rows(u_ref[slot * 2 + half], carry_ref[idx], cw_ref[idx], cb_ref[idx], FFN_CONV)
            for k, r in enumerate(rows):
                carry_ref[idx, k:k + 1, :] = r
            act = _silu(y) if half == 0 else act * y
        act_ref[:, _lane_chunk(c, FFN_CHUNK)] = act.astype(BF16)

    def body(p, _):
        c = 2 * p
        project(c + 1, 1)
        gate(c, 0)
        project(c + 2, 0)
        gate(c + 1, 1)
        return 0

    project(0, 0)
    lax.fori_loop(0, (n - 1) // 2, body, 0)
    gate(n - 1, 0)
    ff = jnp.dot(act_ref[...], down_ref[...], preferred_element_type=F32)
    o_ref[...] = _layer_norm(DN_ALPHA * h_ref[...] + ff, g_ref[...], b_ref[...])


def _ffn_ln(h, up, cw, cb, down, g, b, batch, seq):
    t, d = h.shape
    tm = min(TOKEN_TILE, seq)
    nt = seq // tm
    hidden = down.shape[0]
    n = hidden // FFN_CHUNK
    up3 = up.reshape(d, 2 * n, FFN_CHUNK).transpose(1, 0, 2)
    cw3 = cw.reshape(FFN_CONV, 2 * n, FFN_CHUNK).transpose(1, 0, 2)
    cb3 = cb.reshape(2 * n, 1, FFN_CHUNK)
    row = lambda bi, i: (bi * nt + i, 0)
    return pl.pallas_call(
        _ffn_kernel,
        grid=(batch, nt),
        in_specs=[pl.BlockSpec((tm, d), row), _resident(up3.shape), _resident(cw3.shape), _resident(cb3.shape),
                  _resident(down.shape), _resident(g.shape), _resident(b.shape)],
        out_specs=pl.BlockSpec((tm, d), row),
        out_shape=jax.ShapeDtypeStruct((t, d), F32),
        scratch_shapes=[pltpu.VMEM((2 * n, SUBLANES, FFN_CHUNK), F32), pltpu.VMEM((tm, d), BF16),
                        pltpu.VMEM((4, tm, FFN_CHUNK), F32), pltpu.VMEM((tm, hidden), BF16)],
        compiler_params=_params("arbitrary", "arbitrary"),
        name="ffn_ln",
    )(h, up3, cw3, cb3, down, g, b)


def _rope_kernel(pos_ref, invf_ref, cos_ref, sin_ref):
    ang = pos_ref[...].astype(F32) * invf_ref[0]
    cos_ref[0] = jnp.cos(ang)
    sin_ref[0] = jnp.sin(ang)


def _rope_tables(pos2, invf):
    nf = invf.shape[0]
    r = pos2.shape[0]
    shape = jax.ShapeDtypeStruct((nf, r, LANES), F32)
    blk = pl.BlockSpec((1, r, LANES), lambda f: (f, 0, 0))
    return pl.pallas_call(
        _rope_kernel,
        grid=(nf,),
        in_specs=[pl.BlockSpec((r, LANES), lambda f: (0, 0)), pl.BlockSpec((1, 1, LANES), lambda f: (f, 0, 0))],
        out_specs=[blk, blk],
        out_shape=[shape, shape],
        compiler_params=_params("arbitrary"),
        name="rope_tables",
    )(pos2, invf)


def _rope_combine(t):
    return t + pltpu.roll(t, MLA_ROPE, 1)


def _mla_kv_kernel(h_ref, wd_ref, ng_ref, wk_ref, wv_ref, cs_ref, kn_ref, kr_ref, v_ref):
    hb = h_ref[...].astype(BF16)
    ckv = jnp.dot(hb, wd_ref[...], preferred_element_type=F32)
    c = ckv[:, :MLA_KV_RANK]
    c = c * lax.rsqrt(jnp.mean(c * c, axis=-1, keepdims=True) + RMS_EPS) * ng_ref[...]
    cb = c.astype(BF16)
    kn_ref[...] = jnp.dot(cb, wk_ref[...], preferred_element_type=F32).astype(BF16)
    v_ref[...] = jnp.dot(cb, wv_ref[...], preferred_element_type=F32).astype(BF16)
    kr = _rope_combine(ckv[:, MLA_KV_RANK:] * cs_ref[...])
    lane = lax.broadcasted_iota(jnp.int32, kr.shape, 1)
    kr_ref[...] = jnp.where(lane < MLA_ROPE, kr, 0.0).astype(BF16)


def _mla_kv(h, wd, ng, wk, wv, cs):
    t, d = h.shape
    tm = min(TOKEN_TILE, t)
    row = lambda i: (i, 0)
    n = wk.shape[1]
    return pl.pallas_call(
        _mla_kv_kernel,
        grid=(t // tm,),
        in_specs=[pl.BlockSpec((tm, d), row), _resident(wd.shape), _resident(ng.shape), _resident(wk.shape),
                  _resident(wv.shape), pl.BlockSpec((tm, LANES), row)],
        out_specs=[pl.BlockSpec((tm, n), row), pl.BlockSpec((tm, LANES), row), pl.BlockSpec((tm, n), row)],
        out_shape=[jax.ShapeDtypeStruct((t, n), BF16), jax.ShapeDtypeStruct((t, LANES), BF16),
                   jax.ShapeDtypeStruct((t, n), BF16)],
        compiler_params=_params("arbitrary"),
        name="mla_kv",
    )(h, wd, ng, wk, wv, cs)


def _mla_q_kernel(h_ref, wd_ref, ng_ref, wu_ref, cs_ref, qn_ref, qr_ref):
    scale = (MLA_NOPE + MLA_ROPE) ** -0.5
    hb = h_ref[...].astype(BF16)
    c = jnp.dot(hb, wd_ref[...], preferred_element_type=F32)
    c = c * lax.rsqrt(jnp.mean(c * c, axis=-1, keepdims=True) + RMS_EPS) * ng_ref[...]
    cb = c.astype(BF16)
    cs = cs_ref[...]
    for hd in range(MLA_N_HEADS):
        q = jnp.dot(cb, wu_ref[:, hd * 2 * LANES:(hd + 1) * 2 * LANES], preferred_element_type=F32)
        out = slice(hd * LANES, (hd + 1) * LANES)
        qn_ref[:, out] = (q[:, :LANES] * scale).astype(BF16)
        qr_ref[:, out] = (_rope_combine(q[:, LANES:] * cs) * scale).astype(BF16)


def _mla_q(h, wd, ng, wu, cs):
    t, d = h.shape
    tm = min(TOKEN_TILE, t)
    row = lambda i: (i, 0)
    n = MLA_N_HEADS * LANES
    return pl.pallas_call(
        _mla_q_kernel,
        grid=(t // tm,),
        in_specs=[pl.BlockSpec((tm, d), row), _resident(wd.shape), _resident(ng.shape), _resident(wu.shape),
                  pl.BlockSpec((tm, LANES), row)],
        out_specs=[pl.BlockSpec((tm, n), row), pl.BlockSpec((tm, n), row)],
        out_shape=[jax.ShapeDtypeStruct((t, n), BF16), jax.ShapeDtypeStruct((t, n), BF16)],
        compiler_params=_params("arbitrary"),
        name="mla_q",
    )(h, wd, ng, wu, cs)


def _attn_kernel(qn_ref, qr_ref, kn_ref, kr_ref, v_ref, o_ref):
    seq = qn_ref.shape[0]
    tq = min(Q_TILE, seq)
    nt_dims = (((1,), (1,)), ((), ()))
    r = lax.broadcasted_iota(jnp.int32, (tq, tq), 0)
    c = lax.broadcasted_iota(jnp.int32, (tq, tq), 1)
    diag_mask = c <= r
    for qi in range(seq // tq):
        lo = qi * tq
        q = jnp.concatenate([qn_ref[lo:lo + tq, :], qr_ref[lo:lo + tq, :]], axis=1)
        k_d = jnp.concatenate([kn_ref[lo:lo + tq, :], kr_ref[lo:lo + tq, :]], axis=1)
        s_d = lax.dot_general(q, k_d, nt_dims, preferred_element_type=F32)
        s_d = jnp.where(diag_mask, s_d, -jnp.inf)
        m = jnp.max(s_d, axis=-1, keepdims=True)
        if qi > 0:
            k_p = jnp.concatenate([kn_ref[:lo, :], kr_ref[:lo, :]], axis=1)
            s_p = lax.dot_general(q, k_p, nt_dims, preferred_element_type=F32)
            m = jnp.maximum(m, jnp.max(s_p, axis=-1, keepdims=True))
        p_d = jnp.exp(s_d - m)
        den = jnp.sum(p_d, axis=-1, keepdims=True)
        acc = jnp.dot(p_d.astype(BF16), v_ref[lo:lo + tq, :], preferred_element_type=F32)
        if qi > 0:
            p_p = jnp.exp(s_p - m)
            den = den + jnp.sum(p_p, axis=-1, keepdims=True)
            acc = acc + jnp.dot(p_p.astype(BF16), v_ref[:lo, :], preferred_element_type=F32)
        o_ref[lo:lo + tq, :] = (acc / den).astype(BF16)


def _attention(qn, qr, kn, kr, v, batch, seq):
    t = qn.shape[0]
    head = lambda b, h: (b, h)
    blk = pl.BlockSpec((seq, LANES), head)
    return pl.pallas_call(
        _attn_kernel,
        grid=(batch, MLA_N_HEADS),
        in_specs=[blk, blk, blk, pl.BlockSpec((seq, LANES), lambda b, h: (b, 0)), blk],
        out_specs=blk,
        out_shape=jax.ShapeDtypeStruct((t, MLA_N_HEADS * MLA_V), BF16),
        compiler_params=_params("arbitrary", "arbitrary"),
        name="attn",
    )(qn, qr, kn, kr, v)


def _swap_halves(w):
    half = w.shape[-1] // 2
    return jnp.concatenate([w[..., half:], w[..., :half]], axis=-1)


def _row(v):
    return v.reshape(1, -1).astype(F32)


def kernel(x, positions, ssd_in_proj, ssd_conv_w, ssd_conv_b, ssd_dt_bias, ssd_A_log, ssd_D, ssd_norm_g, ssd_out_proj, kv_down_proj, kv_norm_g, kv_up_k, kv_up_v, q_down_proj, q_norm_g, q_up_proj, attn_out_proj, ffn_up, ffn_conv_w, ffn_conv_b, ffn_down, ln_mix_g, ln_mix_b, ln_ffn_g, ln_ffn_b):
    batch, seq, d = x.shape
    t = batch * seq
    n_a = ssd_in_proj.shape[0]
    n_layers = ffn_up.shape[0]
    d_inner = ssd_norm_g.shape[1]
    n_heads = ssd_A_log.shape[1]
    assert seq % SSD_CHUNK == 0 and seq % min(TOKEN_TILE, seq) == 0 and t % LANES == 0
    assert n_heads <= LANES and d_inner == n_heads * SSD_HEAD_DIM

    inv_freq = 1.0 / (ROPE_THETA ** (jnp.arange(0, MLA_ROPE, 2, dtype=F32) / MLA_ROPE))
    invf = jnp.broadcast_to(inv_freq[:, None, None], (MLA_ROPE // 2, 1, LANES))
    cos_t, sin_t = _rope_tables(positions.reshape(t // LANES, LANES), invf)
    cos = cos_t.reshape(MLA_ROPE // 2, t).T
    sin = sin_t.reshape(MLA_ROPE // 2, t).T
    cs = jnp.concatenate([cos, cos, -sin, sin], axis=1)

    h = x.reshape(t, d)
    kn = kr = v = None
    for i in range(n_layers):
        if i < n_a:
            w_in = ssd_in_proj[i].astype(BF16)
            n_xbc = ssd_conv_w.shape[2]
            wz = w_in[:, :d_inner]
            wxbc = w_in[:, d_inner:d_inner + n_xbc]
            wdt = jnp.pad(w_in[:, d_inner + n_xbc:], ((0, 0), (0, LANES - n_heads)))
            dtb = jnp.pad(ssd_dt_bias[i].astype(F32), (0, LANES - n_heads)).reshape(1, LANES)
            alog = jnp.pad(ssd_A_log[i].astype(F32), (0, LANES - n_heads)).reshape(1, LANES)
            zs, xbc, dt = _ssd_in(h, wz, wxbc, wdt, ssd_conv_w[i].astype(F32), _row(ssd_conv_b[i]), dtb,
                                  batch, seq)
            dexp = jnp.repeat(ssd_D[i].astype(F32), SSD_HEAD_DIM).reshape(1, d_inner)
            y = _ssd_scan(xbc, zs, dt, alog, dexp, _row(ssd_norm_g[i]), batch, seq)
            h = _proj_ln(y, ssd_out_proj[i].astype(BF16), h, _row(ln_mix_g[i]), _row(ln_mix_b[i]))
        else:
            j = i - n_a
            qu = q_up_proj[j].reshape(-1, MLA_N_HEADS, MLA_NOPE + MLA_ROPE)
            qu_ext = jnp.concatenate([qu, _swap_halves(qu[..., MLA_NOPE:])], axis=-1)
            qu_ext = qu_ext.reshape(qu.shape[0], MLA_N_HEADS * 2 * LANES).astype(BF16)
            qn, qr = _mla_q(h, q_down_proj[j].astype(BF16), _row(q_norm_g[j]), qu_ext, cs)
            o = _attention(qn, qr, kn, kr, v, batch, seq)
            h = _proj_ln(o, attn_out_proj[j].astype(BF16), h, _row(ln_mix_g[i]), _row(ln_mix_b[i]))
        h = _ffn_ln(h, ffn_up[i].astype(BF16), ffn_conv_w[i].astype(F32), _row(ffn_conv_b[i]),
                    ffn_down[i].astype(BF16), _row(ln_ffn_g[i]), _row(ln_ffn_b[i]), batch, seq)
        if i == n_a - 1:
            wd_ext = jnp.concatenate([kv_down_proj, _swap_halves(kv_down_proj[:, MLA_KV_RANK:])], axis=1)
            kn, kr, v = _mla_kv(h, wd_ext.astype(BF16), _row(kv_norm_g), kv_up_k.astype(BF16),
                                kv_up_v.astype(BF16), cs)
    return h.reshape(batch, seq, d)
```
